```python
import math
import jax, jax.numpy as jnp
from jax import lax
import numpy as np

D_MODEL = 2048
BATCH = 8
SEQ = 4096
DEPTH = 4

SSM_EXPAND = 2
SSM_DI = SSM_EXPAND * D_MODEL
SSM_HEADDIM = 64
SSM_HEADS = SSM_DI // SSM_HEADDIM
SSM_GROUPS = 8
SSM_STATE = 128
SSM_CONV = 4
SSM_CONV_DIM = SSM_DI + 2 * SSM_GROUPS * SSM_STATE
SSM_IN = SSM_DI + SSM_CONV_DIM + SSM_HEADS

GDN_NK = D_MODEL // 128
GDN_NV = 2 * GDN_NK
GDN_DK = 128
GDN_DV = 128
GDN_CONV = 4
GDN_QKV = 2 * GDN_NK * GDN_DK + GDN_NV * GDN_DV
GDN_IN = GDN_QKV + GDN_NV * GDN_DV + 2 * GDN_NV

FFN_DIM = 256 * math.ceil(8 * D_MODEL / 3 / 256)
FFN_CONV = 3

CHUNK = 64
N_MOD = 6
EPS = 1e-6

kernel_name = "hybrid_ssd_gdn_convffn_adaln"


def rmsnorm(x, w, eps=EPS):
    xf = x.astype(jnp.float32)
    xf = xf * lax.rsqrt(jnp.mean(xf * xf, axis=-1, keepdims=True) + eps)
    return xf.astype(x.dtype) * w


def causal_dwconv(x, w, b=None):
    K, C = w.shape
    y = lax.conv_general_dilated(x, w[:, None, :].astype(x.dtype), window_strides=(1,),
                                 padding=[(K - 1, 0)], dimension_numbers=("NWC", "WIO", "NWC"),
                                 feature_group_count=C)
    if b is not None:
        y = y + b
    return y


def to_chunks(t, nc):
    return jnp.moveaxis(t.reshape(t.shape[0], nc, CHUNK, *t.shape[2:]), 1, 0)


def ssd_chunked(xs, dt, A, Bm, Cm):
    b, L, H, P = xs.shape
    G, N = Bm.shape[2], Bm.shape[3]
    Hg = H // G
    nc = L // CHUNK
    xdt = (xs * dt[..., None]).reshape(b, L, G, Hg, P)
    a = (dt * A).reshape(b, L, G, Hg)
    causal = jnp.tril(jnp.ones((CHUNK, CHUNK), dtype=bool))

    def step(state, inp):
        xc, ac, Bc, Cc = inp
        acum = jnp.cumsum(ac, axis=1)
        seg = acum[:, :, None] - acum[:, None]
        Lm = jnp.exp(jnp.where(causal[None, :, :, None, None], seg, -jnp.inf))
        CB = jnp.einsum("bign,bjgn->bijg", Cc, Bc)
        y_diag = jnp.einsum("bijgh,bjghp->bighp", CB[..., None] * Lm, xc)
        y_off = jnp.einsum("bign,bghpn->bighp", Cc, state) * jnp.exp(acum)[..., None]
        decay_end = jnp.exp(acum[:, -1:] - acum)
        state = (state * jnp.exp(acum[:, -1])[..., None, None]
                 + jnp.einsum("bjgn,bjgh,bjghp->bghpn", Bc, decay_end, xc))
        return state, y_diag + y_off

    state0 = jnp.zeros((b, G, Hg, P, N), jnp.float32)
    _, y = lax.scan(step, state0, (to_chunks(xdt, nc), to_chunks(a, nc),
                                   to_chunks(Bm, nc), to_chunks(Cm, nc)))
    return jnp.moveaxis(y, 0, 1).reshape(b, L, H, P)


def gated_delta_chunked(q, k, v, g, beta):
    b, L, H, DK = k.shape
    DV = v.shape[-1]
    nc = L // CHUNK
    incl = jnp.tril(jnp.ones((CHUNK, CHUNK), dtype=bool))
    strict = jnp.tril(jnp.ones((CHUNK, CHUNK), dtype=bool), -1)
    eye = jnp.eye(CHUNK, dtype=jnp.float32)

    def step(S, inp):
        qc, kc, vc, gc, bc = inp
        gcum = jnp.cumsum(gc, axis=1)
        gh = jnp.swapaxes(gcum, 1, 2)
        seg = gh[..., :, None] - gh[..., None, :]
        decay = jnp.exp(jnp.where(incl, seg, -jnp.inf))
        kb = kc * bc[..., None]
        Amat = jnp.where(strict, jnp.einsum("bihd,bjhd->bhij", kb, kc) * decay, 0.0)
        rhs = jnp.concatenate([vc * bc[..., None], kb * jnp.exp(gcum)[..., None]], axis=-1)
        sol = lax.linalg.triangular_solve(eye + Amat, jnp.swapaxes(rhs, 1, 2), left_side=True,
                                          lower=True, unit_diagonal=True)
        u, w = sol[..., :DV], sol[..., DV:]
        v_new = u - jnp.einsum("bhid,bhde->bhie", w, S)
        attn = jnp.einsum("bihd,bjhd->bhij", qc, kc) * decay
        o = (jnp.einsum("bihd,bhde->bhie", qc * jnp.exp(gcum)[..., None], S)
             + jnp.einsum("bhij,bhje->bhie", attn, v_new))
        g_last = gh[..., -1]
        S = (S * jnp.exp(g_last)[..., None, None]
             + jnp.einsum("bjhd,bhj,bhje->bhde", kc, jnp.exp(g_last[..., None] - gh), v_new))
        return S, o

    S0 = jnp.zeros((b, H, DK, DV), jnp.float32)
    _, o = lax.scan(step, S0, (to_chunks(q, nc), to_chunks(k, nc), to_chunks(v, nc),
                               to_chunks(g, nc), to_chunks(beta, nc)))
    return jnp.transpose(o, (1, 0, 3, 2, 4)).reshape(b, L, H, DV)


def mamba2_mixer(h, w_in, conv_w, conv_b, dt_bias, A_log, Dskip, norm_w, w_out):
    b, L, _ = h.shape
    proj = h @ w_in
    z = proj[..., :SSM_DI]
    xBC = proj[..., SSM_DI:SSM_DI + SSM_CONV_DIM]
    dt = proj[..., SSM_DI + SSM_CONV_DIM:]
    xBC = jax.nn.silu(causal_dwconv(xBC, conv_w, conv_b)).astype(jnp.float32)
    GN = SSM_GROUPS * SSM_STATE
    xs = xBC[..., :SSM_DI].reshape(b, L, SSM_HEADS, SSM_HEADDIM)
    Bm = xBC[..., SSM_DI:SSM_DI + GN].reshape(b, L, SSM_GROUPS, SSM_STATE)
    Cm = xBC[..., SSM_DI + GN:].reshape(b, L, SSM_GROUPS, SSM_STATE)
    dt = jax.nn.softplus(dt.astype(jnp.float32) + dt_bias.astype(jnp.float32))
    A = -jnp.exp(A_log.astype(jnp.float32))
    y = ssd_chunked(xs, dt, A, Bm, Cm) + Dskip.astype(jnp.float32)[:, None] * xs
    yg = (y.reshape(b, L, SSM_DI) * jax.nn.silu(z.astype(jnp.float32))).reshape(b, L, SSM_GROUPS, -1)
    yg = yg * lax.rsqrt(jnp.mean(yg * yg, axis=-1, keepdims=True) + 1e-5)
    yg = yg.reshape(b, L, SSM_DI).astype(h.dtype) * norm_w
    return yg @ w_out


def gdn_mixer(h, w_in, conv_w, dt_bias, A_log, norm_w, w_out):
    b, L, _ = h.shape
    proj = h @ w_in
    qkv = jax.nn.silu(causal_dwconv(proj[..., :GDN_QKV], conv_w)).astype(jnp.float32)
    z = proj[..., GDN_QKV:GDN_QKV + GDN_NV * GDN_DV].reshape(b, L, GDN_NV, GDN_DV)
    ba = proj[..., GDN_QKV + GDN_NV * GDN_DV:].astype(jnp.float32)
    bta, a = ba[..., :GDN_NV], ba[..., GDN_NV:]
    nqk = GDN_NK * GDN_DK
    q = qkv[..., :nqk].reshape(b, L, GDN_NK, GDN_DK)
    k = qkv[..., nqk:2 * nqk].reshape(b, L, GDN_NK, GDN_DK)
    v = qkv[..., 2 * nqk:].reshape(b, L, GDN_NV, GDN_DV)
    q = q * lax.rsqrt(jnp.sum(q * q, axis=-1, keepdims=True) + 1e-6) * (GDN_DK ** -0.5)
    k = k * lax.rsqrt(jnp.sum(k * k, axis=-1, keepdims=True) + 1e-6)
    rep = GDN_NV // GDN_NK
    q = jnp.repeat(q, rep, axis=2)
    k = jnp.repeat(k, rep, axis=2)
    beta = jax.nn.sigmoid(bta)
    g = -jnp.exp(A_log.astype(jnp.float32)) * jax.nn.softplus(a + dt_bias.astype(jnp.float32))
    o = gated_delta_chunked(q, k, v, g, beta)
    o = o * lax.rsqrt(jnp.mean(o * o, axis=-1, keepdims=True) + EPS)
    o = (o.astype(h.dtype) * norm_w) * jax.nn.silu(z)
    return o.reshape(b, L, GDN_NV * GDN_DV) @ w_out


def conv_ffn(h, w_up, conv_w, conv_b, w_down):
    u = causal_dwconv(h @ w_up, conv_w, conv_b)
    gate, val = u[..., :FFN_DIM], u[..., FFN_DIM:]
    return (jax.nn.silu(gate) * val) @ w_down


def setup_inputs(seed: int = 0) -> dict:
    key = jax.random.key(seed)
    ks = list(jax.random.split(key, 32))
    nA = (DEPTH + 1) // 2
    nB = DEPTH // 2

    def nrm(shape, scale):
        return scale * jax.random.normal(ks.pop(), shape, jnp.float32)

    def dt_bias_init(shape):
        u = jax.random.uniform(ks.pop(), shape, jnp.float32)
        dt = jnp.exp(u * (math.log(0.1) - math.log(0.001)) + math.log(0.001))
        return dt + jnp.log(-jnp.expm1(-dt))

    def a_log_init(shape):
        return jnp.log(jax.random.uniform(ks.pop(), shape, jnp.float32, 1.0, 16.0))

    d = D_MODEL
    return {
        "x": nrm((BATCH, SEQ, d), 1.0),
        "c": nrm((BATCH, d), 1.0),
        "ada_w": nrm((DEPTH, d, N_MOD * d), 0.5 * d ** -0.5),
        "ada_b": nrm((DEPTH, N_MOD * d), 0.01),
        "norm_mix_w": 1.0 + nrm((DEPTH, d), 0.02),
        "norm_ffn_w": 1.0 + nrm((DEPTH, d), 0.02),
        "ssm_w_in": nrm((nA, d, SSM_IN), d ** -0.5),
        "ssm_conv_w": nrm((nA, SSM_CONV, SSM_CONV_DIM), SSM_CONV ** -0.5),
        "ssm_conv_b": nrm((nA, SSM_CONV_DIM), 0.01),
        "ssm_dt_bias": dt_bias_init((nA, SSM_HEADS)),
        "ssm_A_log": a_log_init((nA, SSM_HEADS)),
        "ssm_D": 1.0 + nrm((nA, SSM_HEADS), 0.1),
        "ssm_norm_w": 1.0 + nrm((nA, SSM_DI), 0.02),
        "ssm_w_out": nrm((nA, SSM_DI, d), SSM_DI ** -0.5),
        "gdn_w_in": nrm((nB, d, GDN_IN), d ** -0.5),
        "gdn_conv_w": nrm((nB, GDN_CONV, GDN_QKV), GDN_CONV ** -0.5),
        "gdn_dt_bias": dt_bias_init((nB, GDN_NV)),
        "gdn_A_log": a_log_init((nB, GDN_NV)),
        "gdn_norm_w": 1.0 + nrm((nB, GDN_DV), 0.02),
        "gdn_w_out": nrm((nB, GDN_NV * GDN_DV, d), (GDN_NV * GDN_DV) ** -0.5),
        "ffn_w_up": nrm((DEPTH, d, 2 * FFN_DIM), d ** -0.5),
        "ffn_conv_w": nrm((DEPTH, FFN_CONV, 2 * FFN_DIM), FFN_CONV ** -0.5),
        "ffn_conv_b": nrm((DEPTH, 2 * FFN_DIM), 0.01),
        "ffn_w_down": nrm((DEPTH, FFN_DIM, d), FFN_DIM ** -0.5),
        "final_norm_w": 1.0 + nrm((d,), 0.02),
    }


def reference(x, c, ada_w, ada_b, norm_mix_w, norm_ffn_w,
              ssm_w_in, ssm_conv_w, ssm_conv_b, ssm_dt_bias, ssm_A_log, ssm_D, ssm_norm_w, ssm_w_out,
              gdn_w_in, gdn_conv_w, gdn_dt_bias, gdn_A_log, gdn_norm_w, gdn_w_out,
              ffn_w_up, ffn_conv_w, ffn_conv_b, ffn_w_down, final_norm_w):
    cs = jax.nn.silu(c)
    for i in range(DEPTH):
        mod = cs @ ada_w[i] + ada_b[i]
        sh_m, sc_m, g_m, sh_f, sc_f, g_f = [m[:, None, :] for m in jnp.split(mod, N_MOD, axis=-1)]
        h = rmsnorm(x, norm_mix_w[i]) * (1.0 + sc_m) + sh_m
        j = i // 2
        if i % 2 == 0:
            y = mamba2_mixer(h, ssm_w_in[j], ssm_conv_w[j], ssm_conv_b[j], ssm_dt_bias[j],
                             ssm_A_log[j], ssm_D[j], ssm_norm_w[j], ssm_w_out[j])
        else:
            y = gdn_mixer(h, gdn_w_in[j], gdn_conv_w[j], gdn_dt_bias[j], gdn_A_log[j],
                          gdn_norm_w[j], gdn_w_out[j])
        x = x + (g_m * y).astype(x.dtype)
        h = rmsnorm(x, norm_ffn_w[i]) * (1.0 + sc_f) + sh_f
        x = x + (g_f * conv_ffn(h, ffn_w_up[i], ffn_conv_w[i], ffn_conv_b[i], ffn_w_down[i])).astype(x.dtype)
    return rmsnorm(x, final_norm_w)
```

```python
import functools
import math

import jax
import jax.numpy as jnp
from jax import lax
from jax.experimental import pallas as pl
from jax.experimental.pallas import tpu as pltpu

F32 = jnp.float32
BF16 = jnp.bfloat16

SSM_GROUPS = 8
GDN_DK = 128
EPS = 1e-6
SSM_NORM_EPS = 1e-5
L2_EPS = 1e-6

SUBLANES = 8
LANES = 128
VMEM_LIMIT_BYTES = 56 * 1024 * 1024

SSD_CHUNK = 128
GDN_CHUNK = 64
GDN_CHUNKS_PER_STEP = 4
ROW_TILE = 1024
COL_TILE = 1024
FFN_COL_TILE = 512
OUT_COL_TILE = 512


def _params(*sem):
    return pltpu.CompilerParams(dimension_semantics=sem, vmem_limit_bytes=VMEM_LIMIT_BYTES)


def _tile(n, want):
    if n <= want:
        return n
    t = want
    while t >= LANES:
        if n % t == 0:
            return t
        t -= LANES
    return n


def _dot(a, b):
    return jnp.dot(a, b, preferred_element_type=F32)


def _dot_nt(a, b):
    return lax.dot_general(a, b, (((1,), (1,)), ((), ())), preferred_element_type=F32)


def _dot_tn(a, b):
    return lax.dot_general(a, b, (((0,), (0,)), ((), ())), preferred_element_type=F32)


def _split3(x):
    hi = x.astype(BF16)
    r = x - hi.astype(F32)
    mid = r.astype(BF16)
    lo = (r - mid.astype(F32)).astype(BF16)
    return (hi, mid, lo)


def _dot3(parts, m):
    return _dot(parts[0], m) + _dot(parts[1], m) + _dot(parts[2], m)


def _silu(x):
    return x * jax.nn.sigmoid(x)


def _softplus(x):
    return jnp.maximum(x, 0.0) + jnp.log1p(jnp.exp(-jnp.abs(x)))


def _mod_kernel(c_ref, w_ref, b_ref, o_ref):
    cs = _silu(c_ref[...])
    o_ref[0] = _dot(cs.astype(BF16), w_ref[0].astype(BF16)) + b_ref[0]


def _modulation(c, ada_w, ada_b):
    depth, d, n = ada_w.shape
    bsz = c.shape[0]
    tn = _tile(n, COL_TILE)
    return pl.pallas_call(
        _mod_kernel,
        grid=(depth, n // tn),
        in_specs=[
            pl.BlockSpec((bsz, d), lambda l, j: (0, 0)),
            pl.BlockSpec((1, d, tn), lambda l, j: (l, 0, j)),
            pl.BlockSpec((1, 1, tn), lambda l, j: (l, 0, j)),
        ],
        out_specs=pl.BlockSpec((1, bsz, tn), lambda l, j: (l, 0, j)),
        out_shape=jax.ShapeDtypeStruct((depth, bsz, n), F32),
        compiler_params=_params("arbitrary", "arbitrary"),
        name="adaln_mod",
    )(c, ada_w, ada_b.reshape(depth, 1, n))


def _norm_modulate(x_ref, nw_ref, sc_ref, sh_ref):
    x = x_ref[...]
    xn = x * lax.rsqrt(jnp.mean(x * x, axis=-1, keepdims=True) + EPS)
    return ((xn * nw_ref[...]) * (1.0 + sc_ref[...]) + sh_ref[...]).astype(BF16)


def _causal_conv(r, prev, cw_ref, cb_ref):
    k = cw_ref.shape[0]
    full = jnp.concatenate([prev, r], axis=0)
    acc = r * cw_ref[k - 1:k, :] + cb_ref[...]
    for s in range(1, k):
        acc = acc + pltpu.roll(full, s, 0)[SUBLANES:, :] * cw_ref[k - 1 - s:k - s, :]
    return acc


def _take_halo(halo_ref, idx, r, first):
    prev = jnp.where(first, 0.0, halo_ref[idx])
    halo_ref[idx] = r[r.shape[0] - SUBLANES:, :]
    return prev


def _mod_spec(d, row):
    return pl.BlockSpec((None, 1, d), lambda i, j: (row(i), 0, 0))


def _proj_kernel(x_ref, nw_ref, sc_ref, sh_ref, w_ref, cw_ref, cb_ref, wt_ref, tp_ref,
                 o_ref, ot_ref, h_scr, halo_scr, *, tiles_per_seq, conv_lo, conv_hi, n_q, n_k, head_dim, n_gate):
    i = pl.program_id(0)
    j = pl.program_id(1)

    @pl.when(j == 0)
    def _():
        hb = _norm_modulate(x_ref, nw_ref, sc_ref, sh_ref)
        h_scr[...] = hb
        raw = _dot(hb, wt_ref[...])
        if n_gate:
            lane = lax.broadcasted_iota(jnp.int32, raw.shape, 1)
            g = -jnp.exp(tp_ref[1:2, :]) * _softplus(raw + tp_ref[0:1, :])
            ot_ref[...] = jnp.where(lane < n_gate, jax.nn.sigmoid(raw), g)
        else:
            ot_ref[...] = _softplus(raw + tp_ref[0:1, :])

    r = _dot(h_scr[...], w_ref[...])
    in_conv = jnp.logical_and(j >= conv_lo, j < conv_hi)

    @pl.when(jnp.logical_not(in_conv))
    def _():
        o_ref[...] = r.astype(o_ref.dtype)

    @pl.when(in_conv)
    def _():
        prev = _take_halo(halo_scr, j, r, (i % tiles_per_seq) == 0)
        a = _silu(_causal_conv(r, prev, cw_ref, cb_ref))
        if n_q + n_k == 0:
            o_ref[...] = a.astype(o_ref.dtype)
        else:
            is_qk = j < conv_lo + n_q + n_k

            @pl.when(jnp.logical_not(is_qk))
            def _():
                o_ref[...] = a.astype(o_ref.dtype)

            @pl.when(is_qk)
            def _():
                scale = jnp.where(j < conv_lo + n_q, head_dim ** -0.5, 1.0).astype(F32)
                for hd in range(a.shape[1] // head_dim):
                    ah = a[:, hd * head_dim:(hd + 1) * head_dim]
                    inv = lax.rsqrt(jnp.sum(ah * ah, axis=-1, keepdims=True) + L2_EPS) * scale
                    o_ref[:, hd * head_dim:(hd + 1) * head_dim] = (ah * inv).astype(o_ref.dtype)


def _in_proj(x2, nw, mod3, mod_row, w_main, conv_w, conv_b, w_tail, tail_par, *, seq_len, layer,
             conv_lo_col, conv_hi_col, n_q_cols=0, n_k_cols=0, head_dim=LANES, n_gate=0):
    m, d = x2.shape
    n = w_main.shape[-1]
    nt = w_tail.shape[-1]
    tm = _tile(seq_len, ROW_TILE)
    tn = _tile(math.gcd(n, conv_lo_col, conv_hi_col, n_q_cols, n_k_cols), COL_TILE)
    kern = functools.partial(
        _proj_kernel, tiles_per_seq=seq_len // tm, conv_lo=conv_lo_col // tn, conv_hi=conv_hi_col // tn,
        n_q=n_q_cols // tn, n_k=n_k_cols // tn, head_dim=head_dim, n_gate=n_gate)
    kc = conv_w.shape[-2]
    tps = seq_len // tm
    return pl.pallas_call(
        kern,
        grid=(m // tm, n // tn),
        in_specs=[
            pl.BlockSpec((tm, d), lambda i, j: (i, 0)),
            pl.BlockSpec((None, 1, d), lambda i, j: (layer, 0, 0)),
            _mod_spec(d, lambda i: mod_row(i // tps, 1)),
            _mod_spec(d, lambda i: mod_row(i // tps, 0)),
            pl.BlockSpec((d, tn), lambda i, j: (0, j)),
            pl.BlockSpec((kc, tn), lambda i, j: (0, j)),
            pl.BlockSpec((1, tn), lambda i, j: (0, j)),
            pl.BlockSpec((d, nt), lambda i, j: (0, 0)),
            pl.BlockSpec(tail_par.shape, lambda i, j: (0, 0)),
        ],
        out_specs=[
            pl.BlockSpec((tm, tn), lambda i, j: (i, j)),
            pl.BlockSpec((tm, nt), lambda i, j: (i, 0)),
        ],
        out_shape=[jax.ShapeDtypeStruct((m, n), BF16), jax.ShapeDtypeStruct((m, nt), F32)],
        scratch_shapes=[pltpu.VMEM((tm, d), BF16), pltpu.VMEM((n // tn, SUBLANES, tn), F32)],
        compiler_params=_params("arbitrary", "arbitrary"),
        name="mixer_in_proj",
    )(x2, nw, mod3, mod3, w_main, conv_w, conv_b, w_tail, tail_par)


def _ffn_up_kernel(x_ref, nw_ref, sc_ref, sh_ref, wg_ref, wv_ref, cwg_ref, cwv_ref, cbg_ref, cbv_ref,
                   o_ref, h_scr, halo_scr, *, tiles_per_seq):
    i = pl.program_id(0)
    j = pl.program_id(1)

    @pl.when(j == 0)
    def _():
        h_scr[...] = _norm_modulate(x_ref, nw_ref, sc_ref, sh_ref)

    hb = h_scr[...]
    first = (i % tiles_per_seq) == 0
    rg = _dot(hb, wg_ref[...])
    gate = _causal_conv(rg, _take_halo(halo_scr, 2 * j, rg, first), cwg_ref, cbg_ref)
    rv = _dot(hb, wv_ref[...])
    val = _causal_conv(rv, _take_halo(halo_scr, 2 * j + 1, rv, first), cwv_ref, cbv_ref)
    o_ref[...] = (_silu(gate) * val).astype(o_ref.dtype)


def _ffn_up(x2, nw, mod3, mod_row, w_up, conv_w, conv_b, *, seq_len, layer):
    m, d = x2.shape
    f = w_up.shape[-1] // 2
    tm = _tile(seq_len, ROW_TILE)
    tn = _tile(f, FFN_COL_TILE)
    nj = f // tn
    kc = conv_w.shape[-2]
    tps = seq_len // tm
    return pl.pallas_call(
        functools.partial(_ffn_up_kernel, tiles_per_seq=tps),
        grid=(m // tm, nj),
        in_specs=[
            pl.BlockSpec((tm, d), lambda i, j: (i, 0)),
            pl.BlockSpec((None, 1, d), lambda i, j: (layer, 0, 0)),
            _mod_spec(d, lambda i: mod_row(i // tps, 4)),
            _mod_spec(d, lambda i: mod_row(i // tps, 3)),
            pl.BlockSpec((None, d, tn), lambda i, j: (layer, 0, j)),
            pl.BlockSpec((None, d, tn), lambda i, j: (layer, 0, j + nj)),
            pl.BlockSpec((None, kc, tn), lambda i, j: (layer, 0, j)),
            pl.BlockSpec((None, kc, tn), lambda i, j: (layer, 0, j + nj)),
            pl.BlockSpec((None, 1, tn), lambda i, j: (layer, 0, j)),
            pl.BlockSpec((None, 1, tn), lambda i, j: (layer, 0, j + nj)),
        ],
        out_specs=pl.BlockSpec((tm, tn), lambda i, j: (i, j)),
        out_shape=jax.ShapeDtypeStruct((m, f), BF16),
        scratch_shapes=[pltpu.VMEM((tm, d), BF16), pltpu.VMEM((2 * nj, SUBLANES, tn), F32)],
        compiler_params=_params("arbitrary", "arbitrary"),
        name="ffn_up",
    )(x2, nw, mod3, mod3, w_up, w_up, conv_w, conv_w, conv_b, conv_b)


def _out_proj_kernel(y_ref, w_ref, x_ref, g_ref, o_ref):
    o_ref[...] = x_ref[...] + g_ref[...] * _dot(y_ref[...], w_ref[...])


def _out_proj(y, w, x2, mod3, mod_row, *, seq_len, layer, gate_idx, row_tile):
    m, k = y.shape
    d = x2.shape[-1]
    tm = _tile(seq_len, row_tile)
    tn = _tile(d, OUT_COL_TILE)
    tps = seq_len // tm
    return pl.pallas_call(
        _out_proj_kernel,
        grid=(m // tm, d // tn),
        in_specs=[
            pl.BlockSpec((tm, k), lambda i, j: (i, 0)),
            pl.BlockSpec((None, k, tn), lambda i, j: (layer, 0, j)),
            pl.BlockSpec((tm, tn), lambda i, j: (i, j)),
            pl.BlockSpec((None, 1, tn), lambda i, j: (mod_row(i // tps, gate_idx), 0, j)),
        ],
        out_specs=pl.BlockSpec((tm, tn), lambda i, j: (i, j)),
        out_shape=jax.ShapeDtypeStruct((m, d), F32),
        input_output_aliases={2: 0},
        compiler_params=_params("arbitrary", "arbitrary"),
        name="out_proj",
    )(y, w, x2, mod3)


def _final_norm_kernel(x_ref, w_ref, o_ref):
    x = x_ref[...]
    o_ref[...] = (x * lax.rsqrt(jnp.mean(x * x, axis=-1, keepdims=True) + EPS)) * w_ref[...]


def _final_norm(x2, w):
    m, d = x2.shape
    tm = _tile(m, ROW_TILE)
    return pl.pallas_call(
        _final_norm_kernel,
        grid=(m // tm,),
        in_specs=[pl.BlockSpec((tm, d), lambda i: (i, 0)), pl.BlockSpec((1, d), lambda i: (0, 0))],
        out_specs=pl.BlockSpec((tm, d), lambda i: (i, 0)),
        out_shape=jax.ShapeDtypeStruct((m, d), F32),
        compiler_params=_params("arbitrary"),
        name="final_norm",
    )(x2, w.reshape(1, d))


def _ssd_kernel(zx_ref, dt_ref, alog_ref, dskip_ref, nw_ref, e_ref, y_ref, state_scr, *, q, nh, p, ng, ns):
    di = nh * p
    hg = nh // ng
    gw = hg * p

    @pl.when(pl.program_id(1) == 0)
    def _():
        state_scr[...] = jnp.zeros_like(state_scr)

    dt = dt_ref[...]
    a = dt * (-jnp.exp(alog_ref[...]))
    rows = lax.broadcasted_iota(jnp.int32, (q, q), 0)
    cols = lax.broadcasted_iota(jnp.int32, (q, q), 1)
    causal = rows >= cols
    a_parts = _split3(a)
    lower = causal.astype(BF16)
    upper = (rows <= cols).astype(BF16)
    acum = _dot(lower, a_parts[0]) + _dot(lower, a_parts[1]) + _dot(lower, a_parts[2])
    acum_t = _dot_tn(a_parts[0], upper) + _dot_tn(a_parts[1], upper) + _dot_tn(a_parts[2], upper)
    dt_parts = _split3(dt)
    ac_parts = _split3(acum)
    lane = lax.broadcasted_iota(jnp.int32, (q, 2 * p), 1)

    for g in range(ng):
        c0 = g * gw
        e_g = e_ref[:, c0:c0 + gw]
        dt_g = _dot3(dt_parts, e_g)
        ac_g = _dot3(ac_parts, e_g)
        z_g = zx_ref[:, c0:c0 + gw].astype(F32)
        xs_g = zx_ref[:, di + c0:di + c0 + gw].astype(F32)
        b_g = zx_ref[:, 2 * di + g * ns:2 * di + (g + 1) * ns]
        c_g = zx_ref[:, 2 * di + (ng + g) * ns:2 * di + (ng + g + 1) * ns]
        last = ac_g[q - 1:q, :]
        xdt = xs_g * dt_g
        cb = _dot_nt(c_g, b_g)
        st = state_scr[g]
        y_off = _dot(c_g, st.astype(BF16)) * jnp.exp(ac_g)
        y_parts = []
        for pr in range(hg // 2):
            decayed = []
            for hh in (2 * pr, 2 * pr + 1):
                h = g * hg + hh
                seg = acum[:, h:h + 1] - acum_t[h:h + 1, :]
                lm = jnp.exp(jnp.where(causal, seg, -jnp.inf))
                decayed.append((cb * lm).astype(BF16))
            xp = xdt[:, 2 * p * pr:2 * p * (pr + 1)]
            x_bd = jnp.concatenate([jnp.where(lane < p, xp, 0.0), jnp.where(lane >= p, xp, 0.0)], axis=0)
            y_parts.append(_dot(jnp.concatenate(decayed, axis=1), x_bd.astype(BF16)))
        y = jnp.concatenate(y_parts, axis=1) + y_off + dskip_ref[:, c0:c0 + gw] * xs_g
        x_dec = (xdt * jnp.exp(last - ac_g)).astype(BF16)
        state_scr[g] = st * jnp.exp(last) + _dot_tn(b_g, x_dec)
        yg = y * _silu(z_g)
        yn = yg * lax.rsqrt(jnp.mean(yg * yg, axis=-1, keepdims=True) + SSM_NORM_EPS)
        y_ref[:, c0:c0 + gw] = (yn * nw_ref[:, c0:c0 + gw]).astype(y_ref.dtype)


def _ssd_core(zx, dt, a_log, dskip, norm_w, *, bsz, seq_len, layer_j):
    m, n = zx.shape
    nh = dt.shape[-1]
    di = norm_w.shape[-1]
    p = di // nh
    ng = SSM_GROUPS
    ns = (n - 2 * di) // (2 * ng)
    assert p * 2 == LANES and (nh // ng) % 2 == 0
    q = _tile(seq_len, SSD_CHUNK)
    nt = seq_len // q
    expand = (jnp.arange(di)[None, :] // p == jnp.arange(nh)[:, None]).astype(BF16)
    dskip_x = jnp.repeat(dskip[layer_j].astype(F32), p).reshape(1, di)
    return pl.pallas_call(
        functools.partial(_ssd_kernel, q=q, nh=nh, p=p, ng=ng, ns=ns),
        grid=(bsz, nt),
        in_specs=[
            pl.BlockSpec((q, n), lambda b, t: (b * nt + t, 0)),
            pl.BlockSpec((q, nh), lambda b, t: (b * nt + t, 0)),
            pl.BlockSpec((None, 1, nh), lambda b, t: (layer_j, 0, 0)),
            pl.BlockSpec((1, di), lambda b, t: (0, 0)),
            pl.BlockSpec((None, 1, di), lambda b, t: (layer_j, 0, 0)),
            pl.BlockSpec((nh, di), lambda b, t: (0, 0)),
        ],
        out_specs=pl.BlockSpec((q, di), lambda b, t: (b * nt + t, 0)),
        out_shape=jax.ShapeDtypeStruct((m, di), BF16),
        scratch_shapes=[pltpu.VMEM((ng, ns, (nh // ng) * p), F32)],
        compiler_params=_params("arbitrary", "arbitrary"),
        name="ssd_core",
    )(zx, dt, a_log.reshape(a_log.shape[0], 1, nh), dskip_x, norm_w.reshape(norm_w.shape[0], 1, di), expand)


def _gdn_kernel(qkvz_ref, bg_ref, e_ref, nw_ref, o_ref,
                s_scr, gx_scr, bx_scr, grow_scr, a_scr, at_scr, tt_scr, attn_scr,
                *, c, nch, nk, nv, dk, dv, pl_lanes):
    tb = c * nch
    rep = nv // nk
    npair = nv * nch
    shift = c.bit_length() - 1
    koff = nk * dk
    voff = 2 * nk * dk
    zoff = voff + nv * dv

    @pl.when(pl.program_id(1) == 0)
    def _():
        s_scr[...] = jnp.zeros_like(s_scr)
        a_scr[...] = jnp.zeros_like(a_scr)

    rows = lax.broadcasted_iota(jnp.int32, (tb, tb), 0)
    cols = lax.broadcasted_iota(jnp.int32, (tb, tb), 1)
    same = lax.shift_right_logical(rows, shift) == lax.shift_right_logical(cols, shift)
    lower = jnp.logical_and(same, cols <= rows).astype(BF16)
    upper = jnp.logical_and(same, rows <= cols).astype(BF16)
    bg_parts = _split3(bg_ref[...])
    gcum = _dot(lower, bg_parts[0]) + _dot(lower, bg_parts[1]) + _dot(lower, bg_parts[2])
    g_t = _dot_tn(bg_parts[0], upper) + _dot_tn(bg_parts[1], upper) + _dot_tn(bg_parts[2], upper)
    for r in range(nv):
        grow_scr[r] = g_t[nv + r:nv + r + 1, :]
    bx_scr[...] = _dot3(bg_parts, e_ref[0])
    gx_scr[...] = _dot3(_split3(gcum), e_ref[1])

    ri = lax.broadcasted_iota(jnp.int32, (c, c), 0)
    ci = lax.broadcasted_iota(jnp.int32, (c, c), 1)
    incl = ri >= ci
    strict = ri > ci

    def phase_a(kh, carry):
        for ch in range(nch):
            rs = slice(ch * c, (ch + 1) * c)
            k_c = qkvz_ref[rs, pl.ds(pl.multiple_of(koff + kh * dk, LANES), dk)]
            q_c = qkvz_ref[rs, pl.ds(pl.multiple_of(kh * dk, LANES), dk)]
            kq = _dot_nt(jnp.concatenate([k_c, q_c], axis=0), k_c)
            for vv in range(rep):
                h = kh * rep + vv
                hl = pl.ds(pl.multiple_of(h * LANES, LANES), LANES)
                seg = gx_scr[rs, hl][:, :c] - grow_scr[h][:, ch * c:(ch + 1) * c]
                decay = jnp.exp(jnp.where(incl, seg, -jnp.inf))
                beta_col = bx_scr[rs, hl][:, :c]
                amat = jnp.where(strict, beta_col * kq[:c] * decay, 0.0)
                pair = h * nch + ch
                a_scr[pl.ds(pair, c, stride=pl_lanes), 0:c] = amat
                attn_scr[pair] = (kq[c:] * decay).astype(BF16)
        return carry

    lax.fori_loop(0, nk, phase_a, 0)

    for i in range(c):
        at_scr[i] = a_scr[i * pl_lanes:(i + 1) * pl_lanes, :].T
    sub = lax.broadcasted_iota(jnp.int32, (SUBLANES, pl_lanes), 0)
    for i in range(c):
        nb = i // SUBLANES + 1
        acc = [jnp.zeros((SUBLANES, pl_lanes), F32) for _ in range(nb)]
        acc[nb - 1] = jnp.where(sub == i % SUBLANES, 1.0, 0.0)
        for mm in range(i):
            a_im = at_scr[i, mm:mm + 1, :]
            for jb in range(mm // SUBLANES + 1):
                acc[jb] = acc[jb] - a_im * tt_scr[mm, jb * SUBLANES:(jb + 1) * SUBLANES, :]
        for jb in range(c // SUBLANES):
            tt_scr[i, jb * SUBLANES:(jb + 1) * SUBLANES, :] = (
                acc[jb] if jb < nb else jnp.zeros((SUBLANES, pl_lanes), F32))
    for i in range(c):
        a_scr[pl.ds(i, pl_lanes, stride=c), :] = tt_scr[i].T

    heads_per_iter = 4 if nk % 4 == 0 else 1

    def phase_c(ch):
        rs = slice(ch * c, (ch + 1) * c)

        def body(kb, carry):
            for kk in range(heads_per_iter):
                kh = kb * heads_per_iter + kk
                k_c = qkvz_ref[rs, pl.ds(pl.multiple_of(koff + kh * dk, LANES), dk)].astype(F32)
                q_c = qkvz_ref[rs, pl.ds(pl.multiple_of(kh * dk, LANES), dk)].astype(F32)
                for vv in range(rep):
                    h = kh * rep + vv
                    hl = pl.ds(pl.multiple_of(h * LANES, LANES), LANES)
                    pair = h * nch + ch
                    gcol = gx_scr[rs, hl]
                    bcol = bx_scr[rs, hl]
                    eg = jnp.exp(gcol)
                    glast = gcol[c - 1:c, :]
                    v_c = qkvz_ref[rs, pl.ds(pl.multiple_of(voff + h * dv, LANES), dv)].astype(F32)
                    z_c = qkvz_ref[rs, pl.ds(pl.multiple_of(zoff + h * dv, LANES), dv)].astype(F32)
                    t_mat = a_scr[pl.ds(pl.multiple_of(pair * c, c), c), :].astype(BF16)
                    rhs = jnp.concatenate([v_c * bcol, k_c * (bcol * eg)], axis=1).astype(BF16)
                    sol = _dot(t_mat, rhs)
                    u = sol[:, :dv]
                    w = sol[:, dv:]
                    s_h = s_scr[h]
                    wq = jnp.concatenate([w, q_c * eg], axis=0).astype(BF16)
                    ws = _dot(wq, s_h.astype(BF16))
                    v_new = (u - ws[:c]).astype(BF16)
                    o = ws[c:] + _dot(attn_scr[pair], v_new)
                    k_dec = (k_c * jnp.exp(glast - gcol)).astype(BF16)
                    s_scr[h] = s_h * jnp.exp(glast[:, :dv]) + _dot_tn(k_dec, v_new)
                    on = o * lax.rsqrt(jnp.mean(o * o, axis=-1, keepdims=True) + EPS)
                    out = (on * nw_ref[...]) * _silu(z_c)
                    o_ref[rs, pl.ds(pl.multiple_of(h * dv, LANES), dv)] = out.astype(o_ref.dtype)
            return carry

        lax.fori_loop(0, nk // heads_per_iter, body, 0)

    for ch in range(nch):
        phase_c(ch)


def _gdn_core(qkvz, bg, norm_w, *, bsz, seq_len, layer_j):
    m, n = qkvz.shape
    nv = bg.shape[-1] // 2
    dv = norm_w.shape[-1]
    dk = GDN_DK
    nk = (n - 2 * nv * dv) // (2 * dk)
    assert dk == LANES and dv == LANES
    c = GDN_CHUNK
    nch = GDN_CHUNKS_PER_STEP
    tb = c * nch
    nt = seq_len // tb
    npair = nv * nch
    pl_lanes = -(-npair // LANES) * LANES
    head_of_lane = jnp.arange(nv * LANES)[None, :] // LANES
    r = jnp.arange(2 * nv)[:, None]
    expand = jnp.stack([(r == head_of_lane), (r - nv == head_of_lane)]).astype(BF16)
    return pl.pallas_call(
        functools.partial(_gdn_kernel, c=c, nch=nch, nk=nk, nv=nv, dk=dk, dv=dv, pl_lanes=pl_lanes),
        grid=(bsz, nt),
        in_specs=[
            pl.BlockSpec((tb, n), lambda b, t: (b * nt + t, 0)),
            pl.BlockSpec((tb, 2 * nv), lambda b, t: (b * nt + t, 0)),
            pl.BlockSpec((2, 2 * nv, nv * LANES), lambda b, t: (0, 0, 0)),
            pl.BlockSpec((None, 1, dv), lambda b, t: (layer_j, 0, 0)),
        ],
        out_specs=pl.BlockSpec((tb, nv * dv), lambda b, t: (b * nt + t, 0)),
        out_shape=jax.ShapeDtypeStruct((m, nv * dv), BF16),
        scratch_shapes=[
            pltpu.VMEM((nv, dk, dv), F32),
            pltpu.VMEM((tb, nv * LANES), F32),
            pltpu.VMEM((tb, nv * LANES), F32),
            pltpu.VMEM((nv, 1, tb), F32),
            pltpu.VMEM((c * pl_lanes, c), F32),
            pltpu.VMEM((c, c, pl_lanes), F32),
            pltpu.VMEM((c, c, pl_lanes), F32),
            pltpu.VMEM((pl_lanes, c, c), BF16),
        ],
        compiler_params=_params("arbitrary", "arbitrary"),
        name="gdn_core",
    )(qkvz, bg, expand, norm_w.reshape(norm_w.shape[0], 1, dv))


def kernel(x, c, ada_w, ada_b, norm_mix_w, norm_ffn_w, ssm_w_in, ssm_conv_w, ssm_conv_b, ssm_dt_bias, ssm_A_log,
           ssm_D, ssm_norm_w, ssm_w_out, gdn_w_in, gdn_conv_w, gdn_dt_bias, gdn_A_log, gdn_norm_w, gdn_w_out,
           ffn_w_up, ffn_conv_w, ffn_conv_b, ffn_w_down, final_norm_w):
    bsz, seq_len, d = x.shape
    depth = ada_w.shape[0]
    n_mod = ada_w.shape[-1] // d
    m = bsz * seq_len

    mod3 = _modulation(c, ada_w, ada_b).reshape(depth * bsz * n_mod, 1, d)
    x2 = x.reshape(m, d)
    nmw = norm_mix_w.reshape(depth, 1, d)
    nfw = norm_ffn_w.reshape(depth, 1, d)

    di = ssm_norm_w.shape[-1]
    nh = ssm_dt_bias.shape[-1]
    ssm_conv_dim = ssm_conv_w.shape[-1]
    nv = gdn_dt_bias.shape[-1]
    dv = gdn_norm_w.shape[-1]
    qkv_dim = gdn_conv_w.shape[-1]
    qk_cols = (qkv_dim - nv * dv) // 2

    ffn_up_b = ffn_w_up.astype(BF16)
    ffn_down_b = ffn_w_down.astype(BF16)
    ssm_out_b = ssm_w_out.astype(BF16)
    gdn_out_b = gdn_w_out.astype(BF16)
    ffn_cb = ffn_conv_b.reshape(depth, 1, ffn_conv_b.shape[-1])

    for i in range(depth):
        def mod_row(b, which, _i=i):
            return (_i * bsz + b) * n_mod + which

        j = i // 2
        if i % 2 == 0:
            w_in = ssm_w_in[j].astype(BF16)
            n_main = di + ssm_conv_dim
            conv_w = jnp.pad(ssm_conv_w[j], ((0, 0), (di, 0)))
            conv_b = jnp.pad(ssm_conv_b[j], ((di, 0),)).reshape(1, n_main)
            zx, dt = _in_proj(x2, nmw, mod3, mod_row, w_in[:, :n_main], conv_w, conv_b, w_in[:, n_main:],
                              ssm_dt_bias[j].reshape(1, nh), seq_len=seq_len, layer=i,
                              conv_lo_col=di, conv_hi_col=n_main)
            y = _ssd_core(zx, dt, ssm_A_log, ssm_D, ssm_norm_w, bsz=bsz, seq_len=seq_len, layer_j=j)
            x2 = _out_proj(y, ssm_out_b, x2, mod3, mod_row, seq_len=seq_len, layer=j, gate_idx=2,
                           row_tile=ROW_TILE)
        else:
            w_in = gdn_w_in[j].astype(BF16)
            n_main = qkv_dim + nv * dv
            conv_w = jnp.pad(gdn_conv_w[j], ((0, 0), (0, nv * dv)))
            conv_b = jnp.zeros((1, n_main), F32)
            zero = jnp.zeros((nv,), F32)
            tail_par = jnp.stack([jnp.concatenate([zero, gdn_dt_bias[j].astype(F32)]),
                                  jnp.concatenate([zero, gdn_A_log[j].astype(F32)])])
            qkvz, bg = _in_proj(x2, nmw, mod3, mod_row, w_in[:, :n_main], conv_w, conv_b, w_in[:, n_main:],
                                tail_par, seq_len=seq_len, layer=i, conv_lo_col=0, conv_hi_col=qkv_dim,
                                n_q_cols=qk_cols, n_k_cols=qk_cols, head_dim=GDN_DK, n_gate=nv)
            y = _gdn_core(qkvz, bg, gdn_norm_w, bsz=bsz, seq_len=seq_len, layer_j=j)
            x2 = _out_proj(y, gdn_out_b, x2, mod3, mod_row, seq_len=seq_len, layer=j, gate_idx=2,
                           row_tile=ROW_TILE)
        act = _ffn_up(x2, nfw, mod3, mod_row, ffn_up_b, ffn_conv_w, ffn_cb, seq_len=seq_len, layer=i)
        x2 = _out_proj(act, ffn_down_b, x2, mod3, mod_row, seq_len=seq_len, layer=i, gate_idx=5,
                       row_tile=ROW_TILE // 2)
    return _final_norm(x2, final_norm_w).reshape(bsz, seq_len, d)
```

```python
import functools
import math

import jax
import jax.numpy as jnp
from jax import lax
from jax.experimental import pallas as pl
from jax.experimental.pallas import tpu as pltpu

F32 = jnp.float32
BF16 = jnp.bfloat16

SSM_GROUPS = 8
GDN_DK = 128
EPS = 1e-6
SSM_NORM_EPS = 1e-5
L2_EPS = 1e-6

SUBLANES = 8
LANES = 128
VMEM_LIMIT_BYTES = 56 * 1024 * 1024

SSD_CHUNK = 128
GDN_CHUNK = 64
GDN_CHUNKS_PER_STEP = 4
GDN_KHEADS_PER_ITER = 16
ROW_TILE = 1024
COL_TILE = 1024
FFN_COL_TILE = 512
OUT_COL_TILE = 512
STRIP = 256


def _params(*sem):
    return pltpu.CompilerParams(dimension_semantics=sem, vmem_limit_bytes=VMEM_LIMIT_BYTES)


def _tile(n, want):
    if n <= want:
        return n
    t = want
    while t >= LANES:
        if n % t == 0:
            return t
        t -= LANES
    return n


def _dot(a, b):
    return jnp.dot(a, b, preferred_element_type=F32)


def _dot_nt(a, b):
    return lax.dot_general(a, b, (((1,), (1,)), ((), ())), preferred_element_type=F32)


def _dot_tn(a, b):
    return lax.dot_general(a, b, (((0,), (0,)), ((), ())), preferred_element_type=F32)


def _split3(x):
    hi = x.astype(BF16)
    r = x - hi.astype(F32)
    mid = r.astype(BF16)
    lo = (r - mid.astype(F32)).astype(BF16)
    return (hi, mid, lo)


def _aligned(start, multiple):
    return start if isinstance(start, int) else pl.multiple_of(start, multiple)


def _silu(x):
    return x * jax.nn.sigmoid(x)


def _softplus(x):
    return jnp.maximum(x, 0.0) + jnp.log1p(jnp.exp(-jnp.abs(x)))


def _mod_kernel(c_ref, w_ref, b_ref, o_ref):
    cs = _silu(c_ref[...])
    o_ref[0] = _dot(cs.astype(BF16), w_ref[0].astype(BF16)) + b_ref[0]


def _modulation(c, ada_w, ada_b):
    depth, d, n = ada_w.shape
    bsz = c.shape[0]
    tn = _tile(n, COL_TILE)
    return pl.pallas_call(
        _mod_kernel,
        grid=(depth, n // tn),
        in_specs=[
            pl.BlockSpec((bsz, d), lambda l, j: (0, 0)),
            pl.BlockSpec((1, d, tn), lambda l, j: (l, 0, j)),
            pl.BlockSpec((1, 1, tn), lambda l, j: (l, 0, j)),
        ],
        out_specs=pl.BlockSpec((1, bsz, tn), lambda l, j: (l, 0, j)),
        out_shape=jax.ShapeDtypeStruct((depth, bsz, n), F32),
        compiler_params=_params("arbitrary", "arbitrary"),
        name="adaln_mod",
    )(c, ada_w, ada_b.reshape(depth, 1, n))


def _norm_modulate(x_ref, nw_ref, sc_ref, sh_ref):
    x = x_ref[...]
    xn = x * lax.rsqrt(jnp.mean(x * x, axis=-1, keepdims=True) + EPS)
    return ((xn * nw_ref[...]) * (1.0 + sc_ref[...]) + sh_ref[...]).astype(BF16)


def _causal_conv(r, halo_ref, idx, first, cw_ref, cb_ref, cs):
    k = cw_ref.shape[0]
    prev = jnp.where(first, 0.0, halo_ref[idx, :, cs])
    halo_ref[idx, :, cs] = r[r.shape[0] - SUBLANES:, :]
    full = jnp.concatenate([prev, r], axis=0)
    acc = r * cw_ref[k - 1:k, cs] + cb_ref[:, cs]
    for s in range(1, k):
        acc = acc + pltpu.roll(full, s, 0)[SUBLANES:, :] * cw_ref[k - 1 - s:k - s, cs]
    return acc


def _strips(width):
    sw = STRIP if width % STRIP == 0 else (LANES if width % LANES == 0 else width)
    return [slice(s, s + sw) for s in range(0, width, sw)]


def _mod_spec(d, row):
    return pl.BlockSpec((None, 1, d), lambda i, j: (row(i), 0, 0))


def _proj_kernel(x_ref, nw_ref, sc_ref, sh_ref, w_ref, cw_ref, cb_ref, wt_ref, tp_ref,
                 o_ref, ot_ref, h_scr, halo_scr, *, tiles_per_seq, conv_lo, conv_hi, n_q, n_k, head_dim, n_gate):
    i = pl.program_id(0)
    j = pl.program_id(1)

    @pl.when(j == 0)
    def _():
        hb = _norm_modulate(x_ref, nw_ref, sc_ref, sh_ref)
        h_scr[...] = hb
        raw = _dot(hb, wt_ref[...])
        if n_gate:
            lane = lax.broadcasted_iota(jnp.int32, raw.shape, 1)
            g = -jnp.exp(tp_ref[1:2, :]) * _softplus(raw + tp_ref[0:1, :])
            ot_ref[...] = jnp.where(lane < n_gate, jax.nn.sigmoid(raw), g)
        else:
            ot_ref[...] = _softplus(raw + tp_ref[0:1, :])

    in_conv = jnp.logical_and(j >= conv_lo, j < conv_hi)
    is_qk = j < conv_lo + n_q + n_k
    first = (i % tiles_per_seq) == 0
    strips = _strips(o_ref.shape[1])

    def conv_act(cs):
        r = _dot(h_scr[...], w_ref[:, cs])
        return _silu(_causal_conv(r, halo_scr, j, first, cw_ref, cb_ref, cs))

    @pl.when(jnp.logical_not(in_conv))
    def _():
        for cs in strips:
            o_ref[:, cs] = _dot(h_scr[...], w_ref[:, cs]).astype(o_ref.dtype)

    @pl.when(jnp.logical_and(in_conv, jnp.logical_not(is_qk)))
    def _():
        for cs in strips:
            o_ref[:, cs] = conv_act(cs).astype(o_ref.dtype)

    if n_q + n_k:
        @pl.when(jnp.logical_and(in_conv, is_qk))
        def _():
            scale = jnp.where(j < conv_lo + n_q, head_dim ** -0.5, 1.0).astype(F32)
            for cs in strips:
                a = conv_act(cs)
                for hd in range(a.shape[1] // head_dim):
                    ah = a[:, hd * head_dim:(hd + 1) * head_dim]
                    inv = lax.rsqrt(jnp.sum(ah * ah, axis=-1, keepdims=True) + L2_EPS) * scale
                    o_ref[:, cs.start + hd * head_dim:cs.start + (hd + 1) * head_dim] = (ah * inv).astype(o_ref.dtype)


def _in_proj(x2, nw, mod3, mod_row, w_main, conv_w, conv_b, w_tail, tail_par, *, seq_len, layer,
             conv_lo_col, conv_hi_col, n_q_cols=0, n_k_cols=0, head_dim=LANES, n_gate=0):
    m, d = x2.shape
    n = w_main.shape[-1]
    nt = w_tail.shape[-1]
    tm = _tile(seq_len, ROW_TILE)
    tn = _tile(math.gcd(n, conv_lo_col, conv_hi_col, n_q_cols, n_k_cols), COL_TILE)
    kern = functools.partial(
        _proj_kernel, tiles_per_seq=seq_len // tm, conv_lo=conv_lo_col // tn, conv_hi=conv_hi_col // tn,
        n_q=n_q_cols // tn, n_k=n_k_cols // tn, head_dim=head_dim, n_gate=n_gate)
    kc = conv_w.shape[-2]
    tps = seq_len // tm
    return pl.pallas_call(
        kern,
        grid=(m // tm, n // tn),
        in_specs=[
            pl.BlockSpec((tm, d), lambda i, j: (i, 0)),
            pl.BlockSpec((None, 1, d), lambda i, j: (layer, 0, 0)),
            _mod_spec(d, lambda i: mod_row(i // tps, 1)),
            _mod_spec(d, lambda i: mod_row(i // tps, 0)),
            pl.BlockSpec((d, tn), lambda i, j: (0, j)),
            pl.BlockSpec((kc, tn), lambda i, j: (0, j)),
            pl.BlockSpec((1, tn), lambda i, j: (0, j)),
            pl.BlockSpec((d, nt), lambda i, j: (0, 0)),
            pl.BlockSpec(tail_par.shape, lambda i, j: (0, 0)),
        ],
        out_specs=[
            pl.BlockSpec((tm, tn), lambda i, j: (i, j)),
            pl.BlockSpec((tm, nt), lambda i, j: (i, 0)),
        ],
        out_shape=[jax.ShapeDtypeStruct((m, n), BF16), jax.ShapeDtypeStruct((m, nt), F32)],
        scratch_shapes=[pltpu.VMEM((tm, d), BF16), pltpu.VMEM((n // tn, SUBLANES, tn), F32)],
        compiler_params=_params("arbitrary", "arbitrary"),
        name="mixer_in_proj",
    )(x2, nw, mod3, mod3, w_main, conv_w, conv_b, w_tail, tail_par)


def _ffn_up_kernel(x_ref, nw_ref, sc_ref, sh_ref, wg_ref, wv_ref, cwg_ref, cwv_ref, cbg_ref, cbv_ref,
                   o_ref, h_scr, halo_scr, *, tiles_per_seq):
    i = pl.program_id(0)
    j = pl.program_id(1)

    @pl.when(j == 0)
    def _():
        h_scr[...] = _norm_modulate(x_ref, nw_ref, sc_ref, sh_ref)

    first = (i % tiles_per_seq) == 0
    for cs in _strips(o_ref.shape[1]):
        gate = _causal_conv(_dot(h_scr[...], wg_ref[:, cs]), halo_scr, 2 * j, first, cwg_ref, cbg_ref, cs)
        val = _causal_conv(_dot(h_scr[...], wv_ref[:, cs]), halo_scr, 2 * j + 1, first, cwv_ref, cbv_ref, cs)
        o_ref[:, cs] = (_silu(gate) * val).astype(o_ref.dtype)


def _ffn_up(x2, nw, mod3, mod_row, w_up, conv_w, conv_b, *, seq_len, layer):
    m, d = x2.shape
    f = w_up.shape[-1] // 2
    tm = _tile(seq_len, ROW_TILE)
    tn = _tile(f, FFN_COL_TILE)
    nj = f // tn
    kc = conv_w.shape[-2]
    tps = seq_len // tm
    return pl.pallas_call(
        functools.partial(_ffn_up_kernel, tiles_per_seq=tps),
        grid=(m // tm, nj),
        in_specs=[
            pl.BlockSpec((tm, d), lambda i, j: (i, 0)),
            pl.BlockSpec((None, 1, d), lambda i, j: (layer, 0, 0)),
            _mod_spec(d, lambda i: mod_row(i // tps, 4)),
            _mod_spec(d, lambda i: mod_row(i // tps, 3)),
            pl.BlockSpec((None, d, tn), lambda i, j: (layer, 0, j)),
            pl.BlockSpec((None, d, tn), lambda i, j: (layer, 0, j + nj)),
            pl.BlockSpec((None, kc, tn), lambda i, j: (layer, 0, j)),
            pl.BlockSpec((None, kc, tn), lambda i, j: (layer, 0, j + nj)),
            pl.BlockSpec((None, 1, tn), lambda i, j: (layer, 0, j)),
            pl.BlockSpec((None, 1, tn), lambda i, j: (layer, 0, j + nj)),
        ],
        out_specs=pl.BlockSpec((tm, tn), lambda i, j: (i, j)),
        out_shape=jax.ShapeDtypeStruct((m, f), BF16),
        scratch_shapes=[pltpu.VMEM((tm, d), BF16), pltpu.VMEM((2 * nj, SUBLANES, tn), F32)],
        compiler_params=_params("arbitrary", "arbitrary"),
        name="ffn_up",
    )(x2, nw, mod3, mod3, w_up, w_up, conv_w, conv_w, conv_b, conv_b)


def _out_proj_kernel(y_ref, w_ref, x_ref, g_ref, o_ref):
    o_ref[...] = x_ref[...] + g_ref[...] * _dot(y_ref[...], w_ref[...])


def _out_proj(y, w, x2, mod3, mod_row, *, seq_len, layer, gate_idx, row_tile):
    m, k = y.shape
    d = x2.shape[-1]
    tm = _tile(seq_len, row_tile)
    tn = _tile(d, OUT_COL_TILE)
    tps = seq_len // tm
    return pl.pallas_call(
        _out_proj_kernel,
        grid=(m // tm, d // tn),
        in_specs=[
            pl.BlockSpec((tm, k), lambda i, j: (i, 0)),
            pl.BlockSpec((None, k, tn), lambda i, j: (layer, 0, j)),
            pl.BlockSpec((tm, tn), lambda i, j: (i, j)),
            pl.BlockSpec((None, 1, tn), lambda i, j: (mod_row(i // tps, gate_idx), 0, j)),
        ],
        out_specs=pl.BlockSpec((tm, tn), lambda i, j: (i, j)),
        out_shape=jax.ShapeDtypeStruct((m, d), F32),
        input_output_aliases={2: 0},
        compiler_params=_params("arbitrary", "arbitrary"),
        name="out_proj",
    )(y, w, x2, mod3)


def _final_norm_kernel(x_ref, w_ref, o_ref):
    x = x_ref[...]
    o_ref[...] = (x * lax.rsqrt(jnp.mean(x * x, axis=-1, keepdims=True) + EPS)) * w_ref[...]


def _final_norm(x2, w):
    m, d = x2.shape
    tm = _tile(m, ROW_TILE)
    return pl.pallas_call(
        _final_norm_kernel,
        grid=(m // tm,),
        in_specs=[pl.BlockSpec((tm, d), lambda i: (i, 0)), pl.BlockSpec((1, d), lambda i: (0, 0))],
        out_specs=pl.BlockSpec((tm, d), lambda i: (i, 0)),
        out_shape=jax.ShapeDtypeStruct((m, d), F32),
        compiler_params=_params("arbitrary"),
        name="final_norm",
    )(x2, w.reshape(1, d))


def _ssd_kernel(zx_ref, dt_ref, alog_ref, dskip_ref, nw_ref, e_ref, y_ref, state_scr, *, q, nh, p, ng, ns):
    di = nh * p
    hg = nh // ng
    gw = hg * p

    @pl.when(pl.program_id(1) == 0)
    def _():
        state_scr[...] = jnp.zeros_like(state_scr)

    dt = dt_ref[...]
    a = dt * (-jnp.exp(alog_ref[...]))
    rows = lax.broadcasted_iota(jnp.int32, (q, q), 0)
    cols = lax.broadcasted_iota(jnp.int32, (q, q), 1)
    causal = rows >= cols
    a_parts = _split3(a)
    lower = causal.astype(BF16)
    upper = (rows <= cols).astype(BF16)
    acum = _dot(lower, a_parts[0]) + _dot(lower, a_parts[1]) + _dot(lower, a_parts[2])
    acum_t = _dot_tn(a_parts[0], upper) + _dot_tn(a_parts[1], upper) + _dot_tn(a_parts[2], upper)
    dt_parts = _split3(dt)
    ac_parts = _split3(acum)
    lane = lax.broadcasted_iota(jnp.int32, (q, 2 * p), 1)

    for g in range(ng):
        c0 = g * gw
        e_g = e_ref[:, c0:c0 + gw]
        dt_g = _dot(dt_parts[0], e_g) + _dot(dt_parts[1], e_g)
        ac_g = _dot(ac_parts[0], e_g) + _dot(ac_parts[1], e_g)
        z_g = zx_ref[:, c0:c0 + gw].astype(F32)
        xs_g = zx_ref[:, di + c0:di + c0 + gw].astype(F32)
        b_g = zx_ref[:, 2 * di + g * ns:2 * di + (g + 1) * ns]
        c_g = zx_ref[:, 2 * di + (ng + g) * ns:2 * di + (ng + g + 1) * ns]
        last = ac_g[q - 1:q, :]
        xdt = xs_g * dt_g
        cb = _dot_nt(c_g, b_g)
        st = state_scr[g]
        y_off = _dot(c_g, st.astype(BF16)) * jnp.exp(ac_g)
        y_parts = []
        for pr in range(hg // 2):
            decayed = []
            for hh in (2 * pr, 2 * pr + 1):
                h = g * hg + hh
                seg = acum[:, h:h + 1] - acum_t[h:h + 1, :]
                lm = jnp.exp(jnp.where(causal, seg, -jnp.inf))
                decayed.append((cb * lm).astype(BF16))
            xp = xdt[:, 2 * p * pr:2 * p * (pr + 1)]
            x_bd = jnp.concatenate([jnp.where(lane < p, xp, 0.0), jnp.where(lane >= p, xp, 0.0)], axis=0)
            y_parts.append(_dot(jnp.concatenate(decayed, axis=1), x_bd.astype(BF16)))
        y = jnp.concatenate(y_parts, axis=1) + y_off + dskip_ref[:, c0:c0 + gw] * xs_g
        x_dec = (xdt * jnp.exp(last - ac_g)).astype(BF16)
        state_scr[g] = st * jnp.exp(last) + _dot_tn(b_g, x_dec)
        yg = y * _silu(z_g)
        yn = yg * lax.rsqrt(jnp.mean(yg * yg, axis=-1, keepdims=True) + SSM_NORM_EPS)
        y_ref[:, c0:c0 + gw] = (yn * nw_ref[:, c0:c0 + gw]).astype(y_ref.dtype)


def _ssd_core(zx, dt, a_log, dskip, norm_w, *, bsz, seq_len, layer_j):
    m, n = zx.shape
    nh = dt.shape[-1]
    di = norm_w.shape[-1]
    p = di // nh
    ng = SSM_GROUPS
    ns = (n - 2 * di) // (2 * ng)
    assert p * 2 == LANES and (nh // ng) % 2 == 0
    q = _tile(seq_len, SSD_CHUNK)
    nt = seq_len // q
    expand = (jnp.arange(di)[None, :] // p == jnp.arange(nh)[:, None]).astype(BF16)
    dskip_x = jnp.repeat(dskip[layer_j].astype(F32), p).reshape(1, di)
    return pl.pallas_call(
        functools.partial(_ssd_kernel, q=q, nh=nh, p=p, ng=ng, ns=ns),
        grid=(bsz, nt),
        in_specs=[
            pl.BlockSpec((q, n), lambda b, t: (b * nt + t, 0)),
            pl.BlockSpec((q, nh), lambda b, t: (b * nt + t, 0)),
            pl.BlockSpec((None, 1, nh), lambda b, t: (layer_j, 0, 0)),
            pl.BlockSpec((1, di), lambda b, t: (0, 0)),
            pl.BlockSpec((None, 1, di), lambda b, t: (layer_j, 0, 0)),
            pl.BlockSpec((nh, di), lambda b, t: (0, 0)),
        ],
        out_specs=pl.BlockSpec((q, di), lambda b, t: (b * nt + t, 0)),
        out_shape=jax.ShapeDtypeStruct((m, di), BF16),
        scratch_shapes=[pltpu.VMEM((ng, ns, (nh // ng) * p), F32)],
        compiler_params=_params("arbitrary", "arbitrary"),
        name="ssd_core",
    )(zx, dt, a_log.reshape(a_log.shape[0], 1, nh), dskip_x, norm_w.reshape(norm_w.shape[0], 1, di), expand)


def _gdn_kernel(qkvz_ref, bg_ref, e_ref, nw_ref, o_ref,
                s0_scr, s1_scr, gx_scr, bx_scr, grow_scr, a_scr, at_scr, tt_scr, attn_scr,
                u_scr, w_scr, vn_scr, qs_scr, *, c, nch, nk, nv, dk, dv, pl_lanes):
    tb = c * nch
    rep = nv // nk
    npair = nv * nch
    shift = c.bit_length() - 1
    koff = nk * dk
    voff = 2 * nk * dk
    zoff = voff + nv * dv

    @pl.when(pl.program_id(1) == 0)
    def _():
        s0_scr[...] = jnp.zeros_like(s0_scr)
        a_scr[...] = jnp.zeros_like(a_scr)

    rows = lax.broadcasted_iota(jnp.int32, (tb, tb), 0)
    cols = lax.broadcasted_iota(jnp.int32, (tb, tb), 1)
    same = lax.shift_right_logical(rows, shift) == lax.shift_right_logical(cols, shift)
    lower = jnp.logical_and(same, cols <= rows).astype(BF16)
    upper = jnp.logical_and(same, rows <= cols).astype(BF16)
    bg_parts = _split3(bg_ref[...])
    gcum = _dot(lower, bg_parts[0]) + _dot(lower, bg_parts[1]) + _dot(lower, bg_parts[2])
    g_t = _dot_tn(bg_parts[0], upper) + _dot_tn(bg_parts[1], upper) + _dot_tn(bg_parts[2], upper)
    for r in range(nv):
        grow_scr[r] = g_t[nv + r:nv + r + 1, :]
    bx_scr[...] = _dot(bg_parts[0], e_ref[0]) + _dot(bg_parts[1], e_ref[0])
    gc_parts = _split3(gcum)
    gx_scr[...] = _dot(gc_parts[0], e_ref[1]) + _dot(gc_parts[1], e_ref[1])

    ri = lax.broadcasted_iota(jnp.int32, (c, c), 0)
    ci = lax.broadcasted_iota(jnp.int32, (c, c), 1)
    incl = ri >= ci
    strict = ri > ci

    def phase_a(kh, carry):
        for ch in range(nch):
            rs = slice(ch * c, (ch + 1) * c)
            k_c = qkvz_ref[rs, pl.ds(_aligned(koff + kh * dk, LANES), dk)]
            q_c = qkvz_ref[rs, pl.ds(_aligned(kh * dk, LANES), dk)]
            kq = _dot_nt(jnp.concatenate([k_c, q_c], axis=0), k_c)
            for vv in range(rep):
                h = kh * rep + vv
                hl = pl.ds(_aligned(h * LANES, LANES), LANES)
                seg = gx_scr[rs, hl][:, :c] - grow_scr[h][:, ch * c:(ch + 1) * c]
                decay = jnp.exp(jnp.where(incl, seg, -jnp.inf))
                beta_col = bx_scr[rs, hl][:, :c]
                amat = jnp.where(strict, beta_col * kq[:c] * decay, 0.0)
                pair = h * nch + ch
                a_scr[pl.ds(_aligned(pair * c, c), c), :] = amat
                attn_scr[pair] = (kq[c:] * decay).astype(BF16)
        return carry

    lax.fori_loop(0, nk, phase_a, 0)

    for i in range(c):
        at_scr[i] = a_scr[pl.ds(i, pl_lanes, stride=c), :].T
    sub = lax.broadcasted_iota(jnp.int32, (SUBLANES, pl_lanes), 0)
    for i in range(c):
        nb = i // SUBLANES + 1
        acc = [jnp.zeros((SUBLANES, pl_lanes), F32) for _ in range(nb)]
        acc[nb - 1] = jnp.where(sub == i % SUBLANES, 1.0, 0.0)
        for mm in range(i):
            a_im = at_scr[i, mm:mm + 1, :]
            for jb in range(mm // SUBLANES + 1):
                acc[jb] = acc[jb] - a_im * tt_scr[mm, jb * SUBLANES:(jb + 1) * SUBLANES, :]
        for jb in range(c // SUBLANES):
            tt_scr[i, jb * SUBLANES:(jb + 1) * SUBLANES, :] = (
                acc[jb] if jb < nb else jnp.zeros((SUBLANES, pl_lanes), F32))
    for i in range(c):
        a_scr[i * pl_lanes:(i + 1) * pl_lanes, :] = tt_scr[i].T

    kpi = GDN_KHEADS_PER_ITER if nk % GDN_KHEADS_PER_ITER == 0 else 1

    def head_loop(stage):
        def body(kb, carry):
            for kk in range(kpi):
                kh = kb * kpi + kk
                for vv in range(rep):
                    stage(kh, kh * rep + vv)
            return carry
        if nk == kpi:
            body(0, 0)
        else:
            lax.fori_loop(0, nk // kpi, body, 0)

    def lanes_of(h):
        return pl.ds(_aligned(h * LANES, LANES), LANES)

    for ch in range(nch):
        rs = slice(ch * c, (ch + 1) * c)
        s_cur, s_nxt = (s0_scr, s1_scr) if ch % 2 == 0 else (s1_scr, s0_scr)

        def solve_stage(kh, h, rs=rs, ch=ch):
            gcol = gx_scr[rs, lanes_of(h)]
            bcol = bx_scr[rs, lanes_of(h)]
            k_c = qkvz_ref[rs, pl.ds(_aligned(koff + kh * dk, LANES), dk)].astype(F32)
            v_c = qkvz_ref[rs, pl.ds(_aligned(voff + h * dv, LANES), dv)].astype(F32)
            t_mat = a_scr[pl.ds(h * nch + ch, c, stride=pl_lanes), :].astype(BF16)
            rhs = jnp.concatenate([v_c * bcol, k_c * (bcol * jnp.exp(gcol))], axis=1).astype(BF16)
            sol = _dot(t_mat, rhs)
            u_scr[h] = sol[:, :dv]
            w_scr[h] = sol[:, dv:].astype(BF16)

        def state_read_stage(kh, h, rs=rs, s_cur=s_cur):
            q_c = qkvz_ref[rs, pl.ds(_aligned(kh * dk, LANES), dk)].astype(F32)
            qe = (q_c * jnp.exp(gx_scr[rs, lanes_of(h)])).astype(BF16)
            ws = _dot(jnp.concatenate([w_scr[h], qe], axis=0), s_cur[h].astype(BF16))
            vn_scr[h] = (u_scr[h] - ws[:c]).astype(BF16)
            qs_scr[h] = ws[c:]

        def output_stage(kh, h, rs=rs, ch=ch, s_cur=s_cur, s_nxt=s_nxt):
            gcol = gx_scr[rs, lanes_of(h)]
            glast = gcol[c - 1:c, :]
            v_new = vn_scr[h]
            o = qs_scr[h] + _dot(attn_scr[h * nch + ch], v_new)
            k_c = qkvz_ref[rs, pl.ds(_aligned(koff + kh * dk, LANES), dk)].astype(F32)
            k_dec = (k_c * jnp.exp(glast - gcol)).astype(BF16)
            s_nxt[h] = s_cur[h] * jnp.exp(glast[:, :dv]) + _dot_tn(k_dec, v_new)
            z_c = qkvz_ref[rs, pl.ds(_aligned(zoff + h * dv, LANES), dv)].astype(F32)
            on = o * lax.rsqrt(jnp.mean(o * o, axis=-1, keepdims=True) + EPS)
            out = (on * nw_ref[...]) * _silu(z_c)
            o_ref[rs, pl.ds(_aligned(h * dv, LANES), dv)] = out.astype(o_ref.dtype)

        head_loop(solve_stage)
        head_loop(state_read_stage)
        head_loop(output_stage)


def _gdn_core(qkvz, bg, norm_w, *, bsz, seq_len, layer_j):
    m, n = qkvz.shape
    nv = bg.shape[-1] // 2
    dv = norm_w.shape[-1]
    dk = GDN_DK
    nk = (n - 2 * nv * dv) // (2 * dk)
    c = GDN_CHUNK
    nch = GDN_CHUNKS_PER_STEP
    assert dk == LANES and dv == LANES and nch % 2 == 0
    tb = c * nch
    nt = seq_len // tb
    npair = nv * nch
    pl_lanes = -(-npair // LANES) * LANES
    head_of_lane = jnp.arange(nv * LANES)[None, :] // LANES
    r = jnp.arange(2 * nv)[:, None]
    expand = jnp.stack([(r == head_of_lane), (r - nv == head_of_lane)]).astype(BF16)
    return pl.pallas_call(
        functools.partial(_gdn_kernel, c=c, nch=nch, nk=nk, nv=nv, dk=dk, dv=dv, pl_lanes=pl_lanes),
        grid=(bsz, nt),
        in_specs=[
            pl.BlockSpec((tb, n), lambda b, t: (b * nt + t, 0)),
            pl.BlockSpec((tb, 2 * nv), lambda b, t: (b * nt + t, 0)),
            pl.BlockSpec((2, 2 * nv, nv * LANES), lambda b, t: (0, 0, 0)),
            pl.BlockSpec((None, 1, dv), lambda b, t: (layer_j, 0, 0)),
        ],
        out_specs=pl.BlockSpec((tb, nv * dv), lambda b, t: (b * nt + t, 0)),
        out_shape=jax.ShapeDtypeStruct((m, nv * dv), BF16),
        scratch_shapes=[
            pltpu.VMEM((nv, dk, dv), F32),
            pltpu.VMEM((nv, dk, dv), F32),
            pltpu.VMEM((tb, nv * LANES), F32),
            pltpu.VMEM((tb, nv * LANES), F32),
            pltpu.VMEM((nv, 1, tb), F32),
            pltpu.VMEM((c * pl_lanes, c), F32),
            pltpu.VMEM((c, c, pl_lanes), F32),
            pltpu.VMEM((c, c, pl_lanes), F32),
            pltpu.VMEM((pl_lanes, c, c), BF16),
            pltpu.VMEM((nv, c, dv), F32),
            pltpu.VMEM((nv, c, dk), BF16),
            pltpu.VMEM((nv, c, dv), BF16),
            pltpu.VMEM((nv, c, dv), F32),
        ],
        compiler_params=_params("arbitrary", "arbitrary"),
        name="gdn_core",
    )(qkvz, bg, expand, norm_w.reshape(norm_w.shape[0], 1, dv))


def kernel(x, c, ada_w, ada_b, norm_mix_w, norm_ffn_w, ssm_w_in, ssm_conv_w, ssm_conv_b, ssm_dt_bias, ssm_A_log,
           ssm_D, ssm_norm_w, ssm_w_out, gdn_w_in, gdn_conv_w, gdn_dt_bias, gdn_A_log, gdn_norm_w, gdn_w_out,
           ffn_w_up, ffn_conv_w, ffn_conv_b, ffn_w_down, final_norm_w):
    bsz, seq_len, d = x.shape
    depth = ada_w.shape[0]
    n_mod = ada_w.shape[-1] // d
    m = bsz * seq_len

    mod3 = _modulation(c, ada_w, ada_b).reshape(depth * bsz * n_mod, 1, d)
    x2 = x.reshape(m, d)
    nmw = norm_mix_w.reshape(depth, 1, d)
    nfw = norm_ffn_w.reshape(depth, 1, d)

    di = ssm_norm_w.shape[-1]
    nh = ssm_dt_bias.shape[-1]
    ssm_conv_dim = ssm_conv_w.shape[-1]
    nv = gdn_dt_bias.shape[-1]
    dv = gdn_norm_w.shape[-1]
    qkv_dim = gdn_conv_w.shape[-1]
    qk_cols = (qkv_dim - nv * dv) // 2

    ffn_up_b = ffn_w_up.astype(BF16)
    ffn_down_b = ffn_w_down.astype(BF16)
    ssm_out_b = ssm_w_out.astype(BF16)
    gdn_out_b = gdn_w_out.astype(BF16)
    ffn_cb = ffn_conv_b.reshape(depth, 1, ffn_conv_b.shape[-1])

    for i in range(depth):
        def mod_row(b, which, _i=i):
            return (_i * bsz + b) * n_mod + which

        j = i // 2
        if i % 2 == 0:
            w_in = ssm_w_in[j].astype(BF16)
            n_main = di + ssm_conv_dim
            conv_w = jnp.pad(ssm_conv_w[j], ((0, 0), (di, 0)))
            conv_b = jnp.pad(ssm_conv_b[j], ((di, 0),)).reshape(1, n_main)
            zx, dt = _in_proj(x2, nmw, mod3, mod_row, w_in[:, :n_main], conv_w, conv_b, w_in[:, n_main:],
                              ssm_dt_bias[j].reshape(1, nh), seq_len=seq_len, layer=i,
                              conv_lo_col=di, conv_hi_col=n_main)
            y = _ssd_core(zx, dt, ssm_A_log, ssm_D, ssm_norm_w, bsz=bsz, seq_len=seq_len, layer_j=j)
            x2 = _out_proj(y, ssm_out_b, x2, mod3, mod_row, seq_len=seq_len, layer=j, gate_idx=2,
                           row_tile=ROW_TILE)
        else:
            w_in = gdn_w_in[j].astype(BF16)
            n_main = qkv_dim + nv * dv
            conv_w = jnp.pad(gdn_conv_w[j], ((0, 0), (0, nv * dv)))
            conv_b = jnp.zeros((1, n_main), F32)
            zero = jnp.zeros((nv,), F32)
            tail_par = jnp.stack([jnp.concatenate([zero, gdn_dt_bias[j].astype(F32)]),
                                  jnp.concatenate([zero, gdn_A_log[j].astype(F32)])])
            qkvz, bg = _in_proj(x2, nmw, mod3, mod_row, w_in[:, :n_main], conv_w, conv_b, w_in[:, n_main:],
                                tail_par, seq_len=seq_len, layer=i, conv_lo_col=0, conv_hi_col=qkv_dim,
                                n_q_cols=qk_cols, n_k_cols=qk_cols, head_dim=GDN_DK, n_gate=nv)
            y = _gdn_core(qkvz, bg, gdn_norm_w, bsz=bsz, seq_len=seq_len, layer_j=j)
            x2 = _out_proj(y, gdn_out_b, x2, mod3, mod_row, seq_len=seq_len, layer=j, gate_idx=2,
                           row_tile=ROW_TILE)
        act = _ffn_up(x2, nfw, mod3, mod_row, ffn_up_b, ffn_conv_w, ffn_cb, seq_len=seq_len, layer=i)
        x2 = _out_proj(act, ffn_down_b, x2, mod3, mod_row, seq_len=seq_len, layer=i, gate_idx=5,
                       row_tile=ROW_TILE)
    return _final_norm(x2, final_norm_w).reshape(bsz, seq_len, d)
```
